```python
import jax
import jax.numpy as jnp
from jax import lax
import numpy as np

D_MODEL = 1024
BATCH = 4
SEQ = 4096
DEPTH = 2
DEC_BATCH = 32
DEC_SEQ = 1
PAST_LEN = 16384
PAGE_SIZE = 128

N_META = 16
SB_HEADS = 8
SB_HEAD_DIM = 64
SB_WIDTH = SB_HEADS * SB_HEAD_DIM
Q_BLOCK = 128
SB_BIAS_INIT = -6.0
CONV_WIDTH = 512
CONV_K = 3
ML_HEADS = 4
ML_HEAD_DIM = 128
ML_WIDTH = ML_HEADS * ML_HEAD_DIM
ML_CHUNK = 128
FORGET_BIAS = 3.0
PEER_HEADS = 8
PEER_NKEYS = 128
PEER_N = PEER_NKEYS * PEER_NKEYS
PEER_QDIM = 256
PEER_HALF = PEER_QDIM // 2
PEER_TOPK = 16
PEER_TOKEN_BLOCK = 256
EPS = 1e-6
MIX_COLS = 3 * SB_WIDTH + 3 * CONV_WIDTH + 4 * ML_WIDTH + 2 * ML_HEADS + 3 * D_MODEL
ML_F_OFFSET = 3 * SB_WIDTH + 3 * CONV_WIDTH + 4 * ML_WIDTH + ML_HEADS

kernel_name = 'hybrid_sb_conv_mlstm_peer_step'


def rmsnorm(x, g):
    xf = x.astype(jnp.float32)
    y = xf * lax.rsqrt(jnp.mean(xf * xf, axis=-1, keepdims=True) + EPS)
    return (y * g.astype(jnp.float32)).astype(x.dtype)


def split_mix(z):
    sizes = (SB_WIDTH,) * 3 + (CONV_WIDTH,) * 3 + (ML_WIDTH,) * 4 + (ML_HEADS,) * 2 + (D_MODEL,) * 3
    return jnp.split(z, np.cumsum(sizes)[:-1].tolist(), axis=-1)


def sb_block(q, k, v, q_pos, k_pos, bias):
    z = (jnp.einsum('bqhd,bkhd->bhqk', q.astype(jnp.float32), k.astype(jnp.float32)) * (SB_HEAD_DIM ** -0.5)
         + bias.astype(jnp.float32)[None, :, None, None])
    visible = k_pos[None, :] < q_pos[:, None]
    log_stay = jnp.where(visible, jax.nn.log_sigmoid(-z), 0.0)
    log_after = lax.cumsum(log_stay, axis=3, reverse=True) - log_stay
    w = jnp.where(visible, jnp.exp(jax.nn.log_sigmoid(z) + log_after), 0.0)
    return jnp.einsum('bhqk,bkhd->bqhd', w, v.astype(jnp.float32)).astype(q.dtype)


def sb_attention_prompt(q, k, v, bias):
    B_, L, H, d = q.shape
    pos = jnp.arange(L)
    out_meta = sb_block(q[:, :N_META], k, v, pos[:N_META], pos, bias)
    n_blk = (L - N_META) // Q_BLOCK

    def one_block(i):
        start = N_META + i * Q_BLOCK
        qb = lax.dynamic_slice_in_dim(q, start, Q_BLOCK, axis=1)
        return sb_block(qb, k, v, start + jnp.arange(Q_BLOCK), pos, bias)

    out = lax.map(one_block, jnp.arange(n_blk))
    out = jnp.moveaxis(out, 0, 1).reshape(B_, n_blk * Q_BLOCK, H, d)
    return jnp.concatenate([out_meta, out], axis=1)


def gather_pages(cache, page_table):
    B_, n_pages = page_table.shape
    rows = cache[page_table]
    return rows.reshape((B_, n_pages * PAGE_SIZE) + cache.shape[2:])


def causal_short_conv(u, prev, w):
    T = u.shape[1]
    up = jnp.concatenate([prev.astype(u.dtype), u], axis=1)
    y = w[0] * up[:, 0:T]
    for j in range(1, CONV_K):
        y = y + w[j] * up[:, j:j + T]
    return y, up[:, T:]


def mlstm_heads(mq, mk, mv, mi, mf):
    B_, T = mq.shape[:2]

    def heads(t):
        return jnp.swapaxes(t.astype(jnp.float32).reshape(B_, T, ML_HEADS, ML_HEAD_DIM), 1, 2)

    li = jnp.swapaxes(mi.astype(jnp.float32), 1, 2)
    lf = jax.nn.log_sigmoid(jnp.swapaxes(mf.astype(jnp.float32), 1, 2))
    return heads(mq), heads(mk) * (ML_HEAD_DIM ** -0.5), heads(mv), li, lf


def mlstm_chunk(state, q, k, v, li, lf):
    C, n, m = state
    Lc = q.shape[2]
    b = jnp.cumsum(lf, axis=-1)
    causal = jnp.tril(jnp.ones((Lc, Lc), dtype=bool))
    D = jnp.where(causal, b[..., :, None] - b[..., None, :] + li[..., None, :], -jnp.inf)
    log_inter = b + m[..., None]
    m_t = jnp.maximum(log_inter, jnp.max(D, axis=-1))
    s = jnp.einsum('bhtk,bhsk->bhts', q, k) * jnp.exp(D - m_t[..., None])
    w_inter = jnp.exp(log_inter - m_t)
    num = jnp.einsum('bhts,bhsv->bhtv', s, v) + w_inter[..., None] * jnp.einsum('bhtk,bhkv->bhtv', q, C)
    den = jnp.sum(s, axis=-1) + w_inter * jnp.einsum('bhtk,bhk->bht', q, n)
    h = num / jnp.maximum(jnp.abs(den), jnp.exp(-m_t))[..., None]
    b_end = b[..., -1]
    log_row = b_end[..., None] - b + li
    m_new = jnp.maximum(b_end + m, jnp.max(log_row, axis=-1))
    a_prev = jnp.exp(b_end + m - m_new)
    a_row = jnp.exp(log_row - m_new[..., None])
    C_new = a_prev[..., None, None] * C + jnp.einsum('bhs,bhsk,bhsv->bhkv', a_row, k, v)
    n_new = a_prev[..., None] * n + jnp.einsum('bhs,bhsk->bhk', a_row, k)
    return (C_new, n_new, m_new), h


def mlstm_prompt(q, k, v, li, lf):
    B_, H, L, d = q.shape
    state = (jnp.zeros((B_, H, d, d), jnp.float32), jnp.zeros((B_, H, d), jnp.float32),
             jnp.zeros((B_, H), jnp.float32))
    state, h_meta = mlstm_chunk(state, q[:, :, :N_META], k[:, :, :N_META], v[:, :, :N_META],
                                li[:, :, :N_META], lf[:, :, :N_META])
    n_c = (L - N_META) // ML_CHUNK

    def chunks(t):
        t = t[:, :, N_META:]
        return jnp.moveaxis(t.reshape((B_, H, n_c, ML_CHUNK) + t.shape[3:]), 2, 0)

    xs = (chunks(q), chunks(k), chunks(v), chunks(li), chunks(lf))
    state, h_rest = lax.scan(lambda st, inp: mlstm_chunk(st, *inp), state, xs)
    h_rest = jnp.moveaxis(h_rest, 0, 2).reshape(B_, H, n_c * ML_CHUNK, d)
    return jnp.concatenate([h_meta, h_rest], axis=2), state


def mlstm_step(state, inp):
    C, n, m = state
    q, k, v, li, lf = inp
    m_new = jnp.maximum(lf + m, li)
    a_prev = jnp.exp(lf + m - m_new)
    a_in = jnp.exp(li - m_new)
    C = a_prev[..., None, None] * C + a_in[..., None, None] * k[..., :, None] * v[..., None, :]
    n = a_prev[..., None] * n + a_in[..., None] * k
    num = jnp.einsum('bhk,bhkv->bhv', q, C)
    den = jnp.einsum('bhk,bhk->bh', q, n)
    h = num / jnp.maximum(jnp.abs(den), jnp.exp(-m_new))[..., None]
    return (C, n, m_new), h


def mlstm_sample(q, k, v, li, lf, C, n, m):
    state = (C.astype(jnp.float32), n.astype(jnp.float32), m.astype(jnp.float32))
    xs = (jnp.moveaxis(q, 2, 0), jnp.moveaxis(k, 2, 0), jnp.moveaxis(v, 2, 0),
          jnp.moveaxis(li, 2, 0), jnp.moveaxis(lf, 2, 0))
    state, h = lax.scan(mlstm_step, state, xs)
    return jnp.moveaxis(h, 0, 2), state


def mlstm_out(h, mo, gain):
    B_, T = mo.shape[:2]
    h = h * lax.rsqrt(jnp.mean(h * h, axis=-1, keepdims=True) + EPS)
    h = jnp.swapaxes(h, 1, 2).reshape(B_, T, ML_WIDTH) * gain.astype(jnp.float32)
    return (h * jax.nn.sigmoid(mo.astype(jnp.float32))).astype(mo.dtype)


def peer(h, wq, subkeys, u_tab, v_tab):
    T, D = h.shape
    blk = min(PEER_TOKEN_BLOCK, T)
    n_blk = -(-T // blk)
    hb = jnp.pad(h, ((0, n_blk * blk - T), (0, 0))).reshape(n_blk, blk, D)
    keys = subkeys.astype(jnp.float32)
    n_cand = PEER_TOPK * PEER_TOPK

    def one_block(xb):
        q = jnp.dot(xb, wq).astype(jnp.float32).reshape(blk, PEER_HEADS, 2, PEER_HALF)
        sc = jnp.einsum('tmpc,pnc->tmpn', q, keys)
        s_half, i_half = lax.top_k(sc, PEER_TOPK)
        cand = (s_half[..., 0, :, None] + s_half[..., 1, None, :]).reshape(blk, PEER_HEADS, n_cand)
        cidx = (i_half[..., 0, :, None] * PEER_NKEYS + i_half[..., 1, None, :]).reshape(blk, PEER_HEADS, n_cand)
        s_top, j = lax.top_k(cand, PEER_TOPK)
        e = jnp.take_along_axis(cidx, j, axis=-1)
        gate = jax.nn.softmax(s_top, axis=-1)
        act = jax.nn.gelu(jnp.einsum('tmed,td->tme', u_tab[e], xb).astype(jnp.float32), approximate=False)
        return jnp.einsum('tme,tmed->td', gate * act, v_tab[e].astype(jnp.float32)).astype(h.dtype)

    return lax.map(one_block, hb).reshape(n_blk * blk, D)[:T]


def merge_and_ffn(x, a, c, mm, ga, gc, gm, lw):
    w_sb, w_cv, w_ml, w_out, norm_ffn, peer_wq, peer_subkeys, peer_u, peer_v = lw[5:14]
    merged = (jax.nn.sigmoid(ga) * (a @ w_sb) + jax.nn.sigmoid(gc) * (c @ w_cv)
              + jax.nn.sigmoid(gm) * (mm @ w_ml))
    x = x + merged @ w_out
    B_, T, D = x.shape
    h = rmsnorm(x, norm_ffn).reshape(B_ * T, D)
    return x + peer(h, peer_wq, peer_subkeys, peer_u, peer_v).reshape(B_, T, D)


def mix_in(x, lw):
    norm_mix, w_in, b_in = lw[0], lw[1], lw[2]
    z = jnp.einsum('btd,dc->btc', rmsnorm(x, norm_mix), w_in) + b_in
    return split_mix(z)


def prompt_layer(x, lw):
    conv_w, ml_norm, sb_bias = lw[3], lw[4], lw[14]
    B_, L, _ = x.shape
    sq, sk, sv, cb, cc, cx, mq, mk, mv, mo, mi, mf, ga, gc, gm = mix_in(x, lw)
    shp = (B_, L, SB_HEADS, SB_HEAD_DIM)
    k, v = sk.reshape(shp), sv.reshape(shp)
    a = sb_attention_prompt(sq.reshape(shp), k, v, sb_bias).reshape(B_, L, SB_WIDTH)
    conv_y, conv_state = causal_short_conv(cc * cx, jnp.zeros((B_, CONV_K - 1, CONV_WIDTH), x.dtype), conv_w)
    hm, (C, n, m) = mlstm_prompt(*mlstm_heads(mq, mk, mv, mi, mf))
    y = merge_and_ffn(x, a, cb * conv_y, mlstm_out(hm, mo, ml_norm), ga, gc, gm, lw)
    return y, k, v, conv_state, C.astype(x.dtype), n.astype(x.dtype), m.astype(x.dtype)


def sample_layer(x, ck, cv, conv_prev, C, n, m, page_table, lw):
    conv_w, ml_norm, sb_bias = lw[3], lw[4], lw[14]
    B_, T, _ = x.shape
    sq, sk, sv, cb, cc, cx, mq, mk, mv, mo, mi, mf, ga, gc, gm = mix_in(x, lw)
    shp = (B_, T, SB_HEADS, SB_HEAD_DIM)
    q, k, v = sq.reshape(shp), sk.reshape(shp), sv.reshape(shp)
    past = page_table.shape[1] * PAGE_SIZE
    k_all = jnp.concatenate([gather_pages(ck, page_table), k.astype(ck.dtype)], axis=1)
    v_all = jnp.concatenate([gather_pages(cv, page_table), v.astype(cv.dtype)], axis=1)
    a = sb_block(q, k_all, v_all, past + jnp.arange(T), jnp.arange(past + T), sb_bias).reshape(B_, T, SB_WIDTH)
    conv_y, conv_state = causal_short_conv(cc * cx, conv_prev, conv_w)
    hm, (C2, n2, m2) = mlstm_sample(*mlstm_heads(mq, mk, mv, mi, mf), C, n, m)
    y = merge_and_ffn(x, a, cb * conv_y, mlstm_out(hm, mo, ml_norm), ga, gc, gm, lw)
    return (y, k, v, conv_state.astype(conv_prev.dtype), C2.astype(C.dtype), n2.astype(n.dtype),
            m2.astype(m.dtype))


def setup_inputs(seed: int = 0) -> dict:
    key = jax.random.key(seed)
    ks = jax.random.split(key, 26)

    def nrm(i, shape, scale):
        return scale * jax.random.normal(ks[i], shape, jnp.float32)

    n_pages = PAST_LEN // PAGE_SIZE
    n_used = DEC_BATCH * n_pages
    n_pool = n_used + n_used // 4
    page_table = jax.random.permutation(ks[0], n_pool)[:n_used].reshape(DEC_BATCH, n_pages).astype(jnp.int32)
    b_in = nrm(13, (DEPTH, MIX_COLS), 0.02).at[:, ML_F_OFFSET:ML_F_OFFSET + ML_HEADS].add(FORGET_BIAS)
    return {
        'x_prompt': nrm(1, (BATCH, SEQ, D_MODEL), 1.0),
        'x_sample': nrm(2, (DEC_BATCH, DEC_SEQ, D_MODEL), 1.0),
        'cache_k': nrm(3, (DEPTH, n_pool, PAGE_SIZE, SB_HEADS, SB_HEAD_DIM), 1.0),
        'cache_v': nrm(4, (DEPTH, n_pool, PAGE_SIZE, SB_HEADS, SB_HEAD_DIM), 1.0),
        'state_conv': nrm(5, (DEPTH, DEC_BATCH, CONV_K - 1, CONV_WIDTH), 1.0),
        'state_C': nrm(6, (DEPTH, DEC_BATCH, ML_HEADS, ML_HEAD_DIM, ML_HEAD_DIM), 0.3),
        'state_n': nrm(7, (DEPTH, DEC_BATCH, ML_HEADS, ML_HEAD_DIM), 0.4),
        'state_m': nrm(8, (DEPTH, DEC_BATCH, ML_HEADS), 0.5),
        'page_table': page_table,
        'meta_tokens': nrm(9, (N_META, D_MODEL), 1.0),
        'norm_mix': 1.0 + nrm(10, (DEPTH, D_MODEL), 0.02),
        'w_in': nrm(11, (DEPTH, D_MODEL, MIX_COLS), D_MODEL ** -0.5),
        'b_in': b_in,
        'sb_bias': SB_BIAS_INIT + nrm(25, (DEPTH, SB_HEADS), 0.1),
        'conv_w': nrm(12, (DEPTH, CONV_K, CONV_WIDTH), CONV_K ** -0.5),
        'ml_norm': 1.0 + nrm(14, (DEPTH, ML_WIDTH), 0.02),
        'w_sb': nrm(15, (DEPTH, SB_WIDTH, D_MODEL), SB_WIDTH ** -0.5),
        'w_cv': nrm(16, (DEPTH, CONV_WIDTH, D_MODEL), CONV_WIDTH ** -0.5),
        'w_ml': nrm(17, (DEPTH, ML_WIDTH, D_MODEL), ML_WIDTH ** -0.5),
        'w_out': nrm(18, (DEPTH, D_MODEL, D_MODEL), D_MODEL ** -0.5),
        'norm_ffn': 1.0 + nrm(19, (DEPTH, D_MODEL), 0.02),
        'peer_wq': nrm(20, (DEPTH, D_MODEL, PEER_HEADS * PEER_QDIM), D_MODEL ** -0.5),
        'peer_subkeys': nrm(21, (DEPTH, 2, PEER_NKEYS, PEER_HALF), PEER_HALF ** -0.5),
        'peer_u': nrm(22, (DEPTH, PEER_N, D_MODEL), D_MODEL ** -0.5),
        'peer_v': nrm(23, (DEPTH, PEER_N, D_MODEL), PEER_HEADS ** -0.5),
        'norm_final': 1.0 + nrm(24, (D_MODEL,), 0.02),
    }


def reference(x_prompt, x_sample, cache_k, cache_v, state_conv, state_C, state_n, state_m, page_table,
              meta_tokens, norm_mix, w_in, b_in, sb_bias, conv_w, ml_norm, w_sb, w_cv, w_ml, w_out,
              norm_ffn, peer_wq, peer_subkeys, peer_u, peer_v, norm_final):
    B_ = x_prompt.shape[0]
    meta = jnp.broadcast_to(meta_tokens[None].astype(x_prompt.dtype), (B_, N_META, D_MODEL))
    xp = jnp.concatenate([meta, x_prompt], axis=1)
    xs = x_sample
    st_p = []
    st_s = []
    for l in range(DEPTH):
        lw = (norm_mix[l], w_in[l], b_in[l], conv_w[l], ml_norm[l], w_sb[l], w_cv[l], w_ml[l], w_out[l],
              norm_ffn[l], peer_wq[l], peer_subkeys[l], peer_u[l], peer_v[l], sb_bias[l])
        xp, *new_p = prompt_layer(xp, lw)
        xs, *new_s = sample_layer(xs, cache_k[l], cache_v[l], state_conv[l], state_C[l], state_n[l],
                                  state_m[l], page_table, lw)
        st_p.append(new_p)
        st_s.append(new_s)
    y_prompt = rmsnorm(xp, norm_final)[:, N_META:]
    y_sample = rmsnorm(xs, norm_final)
    k_prompt = jnp.stack([s[0] for s in st_p])
    v_prompt = jnp.stack([s[1] for s in st_p])
    conv_prompt = jnp.stack([s[2] for s in st_p])
    C_prompt = jnp.stack([s[3] for s in st_p])
    n_prompt = jnp.stack([s[4] for s in st_p])
    m_prompt = jnp.stack([s[5] for s in st_p])
    k_sample = jnp.stack([s[0] for s in st_s])
    v_sample = jnp.stack([s[1] for s in st_s])
    conv_sample = jnp.stack([s[2] for s in st_s])
    C_sample = jnp.stack([s[3] for s in st_s])
    n_sample = jnp.stack([s[4] for s in st_s])
    m_sample = jnp.stack([s[5] for s in st_s])
    return (y_prompt, y_sample, k_prompt, v_prompt, conv_prompt, C_prompt, n_prompt, m_prompt,
            k_sample, v_sample, conv_sample, C_sample, n_sample, m_sample)
```

```python
import functools

import jax
import jax.numpy as jnp
from jax import lax
from jax.experimental import pallas as pl
from jax.experimental.pallas import tpu as pltpu

N_META = 16
SB_HEADS = 8
SB_HEAD_DIM = 64
SB_WIDTH = SB_HEADS * SB_HEAD_DIM
CONV_WIDTH = 512
CONV_K = 3
ML_HEADS = 4
ML_HEAD_DIM = 128
ML_WIDTH = ML_HEADS * ML_HEAD_DIM
PEER_HEADS = 8
PEER_TOPK = 16
PEER_HALF = 128
PAGE_SIZE = 128
EPS = 1e-6

BLK = 128
NEG = -1e30
VMEM_LIMIT = 56 * 1024 * 1024

F32 = jnp.float32
BF16 = jnp.bfloat16
NT = (((1,), (1,)), ((), ()))


def _cp(*sem):
    return pltpu.CompilerParams(dimension_semantics=sem, vmem_limit_bytes=VMEM_LIMIT)


def _split(x):
    hi = x.astype(BF16)
    lo = (x - hi.astype(F32)).astype(BF16)
    return hi, lo


def _softplus(z):
    return jnp.maximum(z, 0.0) + jnp.log(1.0 + jnp.exp(-jnp.abs(z)))


def _log_sigmoid(z):
    return jnp.minimum(z, 0.0) - jnp.log(1.0 + jnp.exp(-jnp.abs(z)))


def _dot(a, b):
    return jnp.dot(a, b, preferred_element_type=F32)


def _mix_kernel(x_ref, g_ref, w_ref, b_ref, z_ref, h_scr):
    @pl.when(pl.program_id(1) == 0)
    def _():
        x = x_ref[...]
        y = x * lax.rsqrt(jnp.mean(x * x, axis=-1, keepdims=True) + EPS)
        h_scr[...] = (y * g_ref[...]).astype(BF16)

    z_ref[...] = _dot(h_scr[...], w_ref[...]) + b_ref[...]


def _mix_in(x, g, w, b, tm, tn):
    t, d = x.shape
    n = w.shape[1]
    return pl.pallas_call(
        _mix_kernel,
        grid=(t // tm, n // tn),
        in_specs=[
            pl.BlockSpec((tm, d), lambda i, j: (i, 0)),
            pl.BlockSpec((1, d), lambda i, j: (0, 0)),
            pl.BlockSpec((d, tn), lambda i, j: (0, j)),
            pl.BlockSpec((1, tn), lambda i, j: (0, j)),
        ],
        out_specs=pl.BlockSpec((tm, tn), lambda i, j: (i, j)),
        out_shape=jax.ShapeDtypeStruct((t, n), F32),
        scratch_shapes=[pltpu.VMEM((tm, d), BF16)],
        compiler_params=_cp("parallel", "arbitrary"),
    )(x, g, w, b)


def _sb_prompt_kernel(bias_ref, q_ref, k_ref, v_ref, o_ref, *, pad0):
    hp = pl.program_id(1)
    qi = pl.program_id(2)
    rowi = lax.broadcasted_iota(jnp.int32, (BLK, BLK), 0)
    coli = lax.broadcasted_iota(jnp.int32, (BLK, BLK), 1)
    mincl = jnp.where(rowi >= coli, 1.0, 0.0).astype(BF16)
    first_head = coli < SB_HEAD_DIM
    q = q_ref[...] * (SB_HEAD_DIM ** -0.5)
    qh = (jnp.where(first_head, q, 0.0).astype(BF16), jnp.where(first_head, 0.0, q).astype(BF16))
    bh = (bias_ref[2 * hp], bias_ref[2 * hp + 1])

    def body(i, carry):
        kj = qi - i
        off = pl.multiple_of(kj * BLK, BLK)
        kb = k_ref[pl.ds(off, BLK), :].astype(BF16)
        vb = v_ref[pl.ds(off, BLK), :].astype(BF16)
        kpos = coli + kj * BLK
        vis = (kpos < rowi + qi * BLK) & (kpos >= pad0)
        out = []
        for h in range(2):
            acc, run = carry[2 * h], carry[2 * h + 1]
            z = lax.dot_general(qh[h], kb, NT, preferred_element_type=F32) + bh[h]
            ls = jnp.where(vis, -_softplus(z), 0.0)
            hi, lo = _split(ls)
            incl = _dot(hi, mincl) + _dot(lo, mincl)
            w = jnp.where(vis, jnp.exp(z + incl + run), 0.0).astype(BF16)
            out += [acc + _dot(w, vb), run + incl[:, 0:1]]
        return tuple(out)

    zero = jnp.zeros((BLK, BLK), F32)
    zrun = jnp.zeros((BLK, 1), F32)
    acc0, _, acc1, _ = lax.fori_loop(0, qi + 1, body, (zero, zrun, zero, zrun))
    o_ref[...] = jnp.where(first_head, acc0, acc1)


def _sb_prompt(z, bias, nb_batch, lp, pad0, col_q, col_k, col_v):
    nb = lp // BLK
    return pl.pallas_call(
        functools.partial(_sb_prompt_kernel, pad0=pad0),
        grid=(nb_batch, SB_WIDTH // BLK, nb),
        in_specs=[
            pl.BlockSpec(memory_space=pltpu.SMEM),
            pl.BlockSpec((BLK, BLK), lambda b, h, i: (b * nb + i, col_q // BLK + h)),
            pl.BlockSpec((lp, BLK), lambda b, h, i: (b, col_k // BLK + h)),
            pl.BlockSpec((lp, BLK), lambda b, h, i: (b, col_v // BLK + h)),
        ],
        out_specs=pl.BlockSpec((BLK, BLK), lambda b, h, i: (b * nb + i, h)),
        out_shape=jax.ShapeDtypeStruct((nb_batch * lp, SB_WIDTH), F32),
        compiler_params=_cp("parallel", "parallel", "arbitrary"),
    )(bias, z, z, z)


def _sb_decode_kernel(pt_ref, bias_ref, q_ref, *refs, npg):
    del pt_ref
    k_refs, v_refs = refs[:npg], refs[npg:2 * npg]
    o_ref, acc_s, run_s = refs[2 * npg:]
    j = pl.program_id(1)

    @pl.when(j == 0)
    def _():
        acc_s[...] = jnp.zeros_like(acc_s)
        run_s[...] = jnp.zeros_like(run_s)

    hrow = lax.broadcasted_iota(jnp.int32, (SB_HEADS, SB_WIDTH), 0)
    lane = lax.broadcasted_iota(jnp.int32, (SB_HEADS, SB_WIDTH), 1)
    own = (lane >= hrow * SB_HEAD_DIM) & (lane < (hrow + 1) * SB_HEAD_DIM)
    q = q_ref[0] * (SB_HEAD_DIM ** -0.5)
    qbd = jnp.where(own, jnp.broadcast_to(q, (SB_HEADS, SB_WIDTH)), 0.0).astype(BF16)
    rowi = lax.broadcasted_iota(jnp.int32, (BLK, BLK), 0)
    coli = lax.broadcasted_iota(jnp.int32, (BLK, BLK), 1)
    mincl = jnp.where(rowi >= coli, 1.0, 0.0).astype(BF16)
    bias = bias_ref[...]
    acc = acc_s[...]
    run = run_s[...]
    for i in range(npg):
        kp = k_refs[i][0].astype(BF16)
        vp = v_refs[i][0].astype(BF16)
        z = lax.dot_general(qbd, kp, NT, preferred_element_type=F32) + bias
        hi, lo = _split(-_softplus(z))
        incl = _dot(hi, mincl) + _dot(lo, mincl)
        w = jnp.exp(z + incl + run).astype(BF16)
        acc = acc + _dot(w, vp)
        run = run + jnp.broadcast_to(incl[:, 0:1], run.shape)
    acc_s[...] = acc
    run_s[...] = run

    @pl.when(j == pl.num_programs(1) - 1)
    def _():
        o_ref[0] = jnp.sum(jnp.where(own, acc, 0.0), axis=0, keepdims=True)


def _sb_decode(page_table, bias_b, q, cache_k, cache_v, layer, n_pool, npg):
    nb, n_pages = page_table.shape
    steps = n_pages // npg

    def page_spec(i):
        def index(b, j, pt):
            return (layer * n_pool + pt[b, n_pages - 1 - (j * npg + i)], 0, 0)
        return pl.BlockSpec((1, PAGE_SIZE, SB_WIDTH), index)

    grid_spec = pltpu.PrefetchScalarGridSpec(
        num_scalar_prefetch=1,
        grid=(nb, steps),
        in_specs=[
            pl.BlockSpec((SB_HEADS, BLK), lambda b, j, pt: (0, 0)),
            pl.BlockSpec((1, 1, SB_WIDTH), lambda b, j, pt: (b, 0, 0)),
        ] + [page_spec(i) for i in range(npg)] * 2,
        out_specs=pl.BlockSpec((1, 1, SB_WIDTH), lambda b, j, pt: (b, 0, 0)),
        scratch_shapes=[pltpu.VMEM((SB_HEADS, SB_WIDTH), F32), pltpu.VMEM((SB_HEADS, BLK), F32)],
    )
    return pl.pallas_call(
        functools.partial(_sb_decode_kernel, npg=npg),
        grid_spec=grid_spec,
        out_shape=jax.ShapeDtypeStruct((nb, 1, SB_WIDTH), F32),
        compiler_params=_cp("parallel", "arbitrary"),
    )(page_table, bias_b, q, *([cache_k] * npg), *([cache_v] * npg))


def _ml_out(hh, gain, o):
    hn = hh * lax.rsqrt(jnp.mean(hh * hh, axis=-1, keepdims=True) + EPS)
    return hn * gain * jax.nn.sigmoid(o)


def _ml_prompt_kernel(cb_ref, cc_ref, cx_ref, q_ref, k_ref, v_ref, o_ref, zif_ref, cw_ref, gain_ref,
                      c_out, hm_out, cst_out, cs_out, ns_out, ms_out, ubuf, c_s, n_s, m_s, *, pad0):
    ci = pl.program_id(1)

    @pl.when(ci == 0)
    def _():
        ubuf[0:8, :] = jnp.zeros((8, CONV_WIDTH), F32)
        c_s[...] = jnp.zeros_like(c_s)
        n_s[...] = jnp.zeros_like(n_s)
        m_s[...] = jnp.zeros_like(m_s)

    real_c = (ci > 0) | (lax.broadcasted_iota(jnp.int32, (BLK, 1), 0) >= pad0)
    real_r = (ci > 0) | (lax.broadcasted_iota(jnp.int32, (1, BLK), 1) >= pad0)

    u = jnp.where(real_c, cc_ref[...] * cx_ref[...], 0.0)
    ubuf[8:8 + BLK, :] = u
    y = cw_ref[0:1, :] * ubuf[6:6 + BLK, :] + cw_ref[1:2, :] * ubuf[7:7 + BLK, :] + cw_ref[2:3, :] * u
    c_out[...] = cb_ref[...] * y
    ubuf[0:8, :] = u[BLK - 8:, :]
    cst_out[0] = u[BLK - 8:, :]

    rowi = lax.broadcasted_iota(jnp.int32, (BLK, BLK), 0)
    coli = lax.broadcasted_iota(jnp.int32, (BLK, BLK), 1)
    ltri = jnp.where(rowi >= coli, 1.0, 0.0).astype(BF16)
    utri = jnp.where(rowi <= coli, 1.0, 0.0).astype(BF16)
    causal = coli <= rowi
    zif = zif_ref[...]
    zif_t = zif.T
    scale = ML_HEAD_DIM ** -0.5
    for h in range(ML_HEADS):
        sl = slice(h * ML_HEAD_DIM, (h + 1) * ML_HEAD_DIM)
        li_c = jnp.where(real_c, zif[:, h:h + 1], NEG)
        lf_c = jnp.where(real_c, _log_sigmoid(zif[:, ML_HEADS + h:ML_HEADS + h + 1]), 0.0)
        li_r = jnp.where(real_r, zif_t[h:h + 1, :], NEG)
        lf_r = jnp.where(real_r, _log_sigmoid(zif_t[ML_HEADS + h:ML_HEADS + h + 1, :]), 0.0)
        ch, cl = _split(jnp.broadcast_to(lf_c, (BLK, BLK)))
        bcol = _dot(ltri, ch) + _dot(ltri, cl)
        rh, rl = _split(jnp.broadcast_to(lf_r, (BLK, BLK)))
        brow = _dot(rh, utri) + _dot(rl, utri)
        m_b = m_s[h]
        dmat = jnp.where(causal, bcol - brow + li_r, NEG)
        mt = jnp.maximum(bcol + m_b, jnp.max(dmat, axis=1, keepdims=True))
        qf = q_ref[:, sl]
        kf = k_ref[:, sl] * scale
        qb, kb, vb = qf.astype(BF16), kf.astype(BF16), v_ref[:, sl].astype(BF16)
        p = lax.dot_general(qb, kb, NT, preferred_element_type=F32) * jnp.exp(dmat - mt)
        w_inter = jnp.exp(bcol + m_b - mt)
        c_old = c_s[h]
        n_old = n_s[h]
        num = _dot(p.astype(BF16), vb) + w_inter * _dot(qb, c_old.astype(BF16))
        den = (jnp.sum(p, axis=1, keepdims=True)
               + w_inter[:, 0:1] * jnp.sum(qf * n_old, axis=1, keepdims=True))
        hh = num / jnp.maximum(jnp.abs(den), jnp.exp(-mt[:, 0:1]))
        hm_out[:, sl] = _ml_out(hh, gain_ref[:, sl], o_ref[:, sl])

        bend = bcol[BLK - 1:BLK, :]
        log_row = bend - brow[0:1, :] + li_r
        m_new = jnp.maximum(bend + m_b, jnp.max(log_row, axis=1, keepdims=True))
        a_prev = jnp.exp(bend + m_b - m_new)
        ka = kf * jnp.exp(bend - bcol + li_c - m_new)
        c_new = a_prev * c_old + _dot(ka.T.astype(BF16), vb)
        n_new = a_prev * n_old + jnp.sum(ka, axis=0, keepdims=True)
        c_s[h] = c_new
        n_s[h] = n_new
        m_s[h] = m_new
        cs_out[0, h] = c_new
        ns_out[0, h:h + 1, :] = n_new
        ms_out[0, h:h + 1, :] = m_new


def _ml_prompt(z, conv_w, gain, nb_batch, lp, pad0, cols):
    nb = lp // BLK

    def zspec(col, width):
        return pl.BlockSpec((BLK, width), lambda b, i: (b * nb + i, col // width))

    row_spec = pl.BlockSpec((BLK, ML_WIDTH), lambda b, i: (b * nb + i, 0))
    return pl.pallas_call(
        functools.partial(_ml_prompt_kernel, pad0=pad0),
        grid=(nb_batch, nb),
        in_specs=[zspec(cols[n], 512) for n in ("cb", "cc", "cx", "mq", "mk", "mv", "mo")] + [
            zspec(cols["if"], BLK),
            pl.BlockSpec((CONV_K, CONV_WIDTH), lambda b, i: (0, 0)),
            pl.BlockSpec((1, ML_WIDTH), lambda b, i: (0, 0)),
        ],
        out_specs=[
            row_spec, row_spec,
            pl.BlockSpec((1, 8, CONV_WIDTH), lambda b, i: (b, 0, 0)),
            pl.BlockSpec((1, ML_HEADS, BLK, BLK), lambda b, i: (b, 0, 0, 0)),
            pl.BlockSpec((1, ML_HEADS, BLK), lambda b, i: (b, 0, 0)),
            pl.BlockSpec((1, ML_HEADS, BLK), lambda b, i: (b, 0, 0)),
        ],
        out_shape=[
            jax.ShapeDtypeStruct((nb_batch * lp, CONV_WIDTH), F32),
            jax.ShapeDtypeStruct((nb_batch * lp, ML_WIDTH), F32),
            jax.ShapeDtypeStruct((nb_batch, 8, CONV_WIDTH), F32),
            jax.ShapeDtypeStruct((nb_batch, ML_HEADS, BLK, BLK), F32),
            jax.ShapeDtypeStruct((nb_batch, ML_HEADS, BLK), F32),
            jax.ShapeDtypeStruct((nb_batch, ML_HEADS, BLK), F32),
        ],
        scratch_shapes=[
            pltpu.VMEM((8 + BLK, CONV_WIDTH), F32),
            pltpu.VMEM((ML_HEADS, BLK, BLK), F32),
            pltpu.VMEM((ML_HEADS, 1, BLK), F32),
            pltpu.VMEM((ML_HEADS, 1, BLK), F32),
        ],
        compiler_params=_cp("parallel", "arbitrary"),
    )(z, z, z, z, z, z, z, z, conv_w, gain)


def _ml_sample_kernel(cb_ref, cc_ref, cx_ref, q_ref, k_ref, v_ref, o_ref, zif_ref, cw_ref, gain_ref,
                      sc_ref, c_ref, n_ref, m_ref, c_out, hm_out, sc_out, cs_out, ns_out, ms_out):
    u = cc_ref[0] * cx_ref[0]
    prev = sc_ref[0]
    y = cw_ref[0:1, :] * prev[0:1, :] + cw_ref[1:2, :] * prev[1:2, :] + cw_ref[2:3, :] * u
    c_out[0] = cb_ref[0] * y
    sc_out[0, 0:1, :] = prev[1:2, :]
    sc_out[0, 1:2, :] = u

    zif = zif_ref[0]
    q_all, k_all, v_all, o_all = q_ref[0], k_ref[0], v_ref[0], o_ref[0]
    scale = ML_HEAD_DIM ** -0.5
    for h in range(ML_HEADS):
        sl = slice(h * ML_HEAD_DIM, (h + 1) * ML_HEAD_DIM)
        li = zif[:, h:h + 1]
        lf = _log_sigmoid(zif[:, ML_HEADS + h:ML_HEADS + h + 1])
        m_b = m_ref[0, h:h + 1, :]
        m_new = jnp.maximum(lf + m_b, li)
        a_prev = jnp.exp(lf + m_b - m_new)
        a_in = jnp.exp(li - m_new)
        q_r, k_r, v_r = q_all[:, sl], k_all[:, sl] * scale, v_all[:, sl]
        k_c = jnp.broadcast_to(k_r, (BLK, BLK)).T
        q_c = jnp.broadcast_to(q_r, (BLK, BLK)).T
        c_new = a_prev * c_ref[0, h] + (a_in * k_c) * v_r
        n_new = a_prev * n_ref[0, h:h + 1, :] + a_in * k_r
        num = jnp.sum(q_c * c_new, axis=0, keepdims=True)
        den = jnp.sum(q_r * n_new, axis=1, keepdims=True)
        hh = num / jnp.maximum(jnp.abs(den), jnp.exp(-m_new))
        hm_out[0, :, sl] = _ml_out(hh, gain_ref[:, sl], o_all[:, sl])
        cs_out[0, h] = c_new
        ns_out[0, h:h + 1, :] = n_new
        ms_out[0, h:h + 1, :] = m_new


def _ml_sample(zs, conv_w, gain, st_conv, st_c, st_n, st_m, cols):
    nb = zs.shape[0]

    def zspec(col, width):
        return pl.BlockSpec((1, 1, width), lambda b: (b, 0, col // width))

    def full(*shape):
        return pl.BlockSpec((1,) + shape, lambda b: (b,) + (0,) * len(shape))

    return pl.pallas_call(
        _ml_sample_kernel,
        grid=(nb,),
        in_specs=[zspec(cols[n], 512) for n in ("cb", "cc", "cx", "mq", "mk", "mv", "mo")] + [
            zspec(cols["if"], BLK),
            pl.BlockSpec((CONV_K, CONV_WIDTH), lambda b: (0, 0)),
            pl.BlockSpec((1, ML_WIDTH), lambda b: (0, 0)),
            full(CONV_K - 1, CONV_WIDTH), full(ML_HEADS, BLK, BLK), full(ML_HEADS, BLK), full(ML_HEADS, BLK),
        ],
        out_specs=[full(1, CONV_WIDTH), full(1, ML_WIDTH), full(CONV_K - 1, CONV_WIDTH),
                   full(ML_HEADS, BLK, BLK), full(ML_HEADS, BLK), full(ML_HEADS, BLK)],
        out_shape=[
            jax.ShapeDtypeStruct((nb, 1, CONV_WIDTH), F32),
            jax.ShapeDtypeStruct((nb, 1, ML_WIDTH), F32),
            jax.ShapeDtypeStruct((nb, CONV_K - 1, CONV_WIDTH), F32),
            jax.ShapeDtypeStruct((nb, ML_HEADS, BLK, BLK), F32),
            jax.ShapeDtypeStruct((nb, ML_HEADS, BLK), F32),
            jax.ShapeDtypeStruct((nb, ML_HEADS, BLK), F32),
        ],
        compiler_params=_cp("parallel"),
    )(zs, zs, zs, zs, zs, zs, zs, zs, conv_w, gain, st_conv, st_c, st_n, st_m)


def _merge_kernel(x_ref, a_ref, c_ref, hm_ref, ga_ref, gc_ref, gm_ref, wsb_ref, wcv_ref, wml_ref,
                  wout_ref, nf_ref, x2_out, h2_out):
    def proj(t_ref, w_ref):
        return _dot(t_ref[...].astype(BF16), w_ref[...])

    merged = (jax.nn.sigmoid(ga_ref[...]) * proj(a_ref, wsb_ref)
              + jax.nn.sigmoid(gc_ref[...]) * proj(c_ref, wcv_ref)
              + jax.nn.sigmoid(gm_ref[...]) * proj(hm_ref, wml_ref))
    x2 = x_ref[...] + _dot(merged.astype(BF16), wout_ref[...])
    x2_out[...] = x2
    h2_out[...] = x2 * lax.rsqrt(jnp.mean(x2 * x2, axis=-1, keepdims=True) + EPS) * nf_ref[...]


def _merge(x, a, c, hm, z, wsb, wcv, wml, wout, nf, tm):
    t, d = x.shape
    row = lambda w: pl.BlockSpec((tm, w), lambda i: (i, 0))
    const = lambda arr: pl.BlockSpec(arr.shape, lambda i: (0, 0))
    gate = lambda g: pl.BlockSpec((tm, d), lambda i: (i, g))
    return pl.pallas_call(
        _merge_kernel,
        grid=(t // tm,),
        in_specs=[row(d), row(SB_WIDTH), row(CONV_WIDTH), row(ML_WIDTH), gate(0), gate(1), gate(2),
                  const(wsb), const(wcv), const(wml), const(wout), const(nf)],
        out_specs=[row(d), row(d)],
        out_shape=[jax.ShapeDtypeStruct((t, d), F32), jax.ShapeDtypeStruct((t, d), F32)],
        compiler_params=_cp("parallel"),
    )(x, a, c, hm, z, z, z, wsb, wcv, wml, wout, nf)


def _top_sorted(x, k):
    n, cols = x.shape
    rows = lax.broadcasted_iota(jnp.int32, (n, cols), 0).astype(F32)
    rank = jnp.full((n, cols), float(k), F32)
    vals = []
    for r in range(k):
        mx = jnp.max(x, axis=0, keepdims=True)
        first = jnp.min(jnp.where(x == mx, rows, float(n)), axis=0, keepdims=True)
        sel = rows == first
        rank = jnp.where(sel, float(r), rank)
        x = jnp.where(sel, -jnp.inf, x)
        vals.append(mx)
    return vals, rank


def _route_kernel(h_ref, wh_ref, wl_ref, kh_ref, kl_ref, r1_out, e1_out, l0_out, a0_out):
    xh, xl = _split(h_ref[...])
    dg = functools.partial(lax.dot_general, dimension_numbers=NT, preferred_element_type=F32)
    q_t = dg(wh_ref[...], xh) + dg(wl_ref[...], xh) + dg(wh_ref[...], xl)
    cols = q_t.shape[1]
    k = PEER_TOPK
    slot = lax.broadcasted_iota(jnp.int32, (k, cols), 0).astype(F32)
    for m in range(PEER_HEADS):
        sc = []
        for p in range(2):
            qh, ql = _split(q_t[(2 * m + p) * PEER_HALF:(2 * m + p + 1) * PEER_HALF, :])
            sc.append(_dot(kh_ref[p], qh) + _dot(kl_ref[p], qh) + _dot(kh_ref[p], ql))
        v0, rank0 = _top_sorted(sc[0], k)
        v1, rank1 = _top_sorted(sc[1], k)
        s0 = jnp.concatenate(v0, axis=0)
        cnt = jnp.zeros((k, cols), F32)
        front = s0 + v1[0]
        zsum = jnp.zeros((1, cols), F32)
        for _ in range(k):
            mx = jnp.max(front, axis=0, keepdims=True)
            first = jnp.min(jnp.where(front == mx, slot, float(k)), axis=0, keepdims=True)
            cnt = jnp.where(slot == first, cnt + 1.0, cnt)
            nxt = jnp.full((k, cols), -jnp.inf, F32)
            for j in range(1, k):
                nxt = jnp.where(cnt == float(j), v1[j], nxt)
            front = jnp.where(cnt == 0.0, front, s0 + nxt)
            zsum = zsum + jnp.exp(mx - (v0[0] + v1[0]))
        picks = jnp.zeros_like(rank0)
        for i in range(k):
            picks = jnp.where(rank0 == float(i), cnt[i:i + 1, :], picks)
        r1_out[m] = rank1
        e1_out[m] = jnp.exp(sc[1] - v1[0])
        l0_out[m] = picks
        a0_out[m] = jnp.exp(sc[0] - v0[0]) / zsum


def _route(h2, wq_hi, wq_lo, key_hi, key_lo, tl):
    t, d = h2.shape
    const = lambda arr: pl.BlockSpec(arr.shape, lambda i: (0,) * arr.ndim)
    out = pl.BlockSpec((PEER_HEADS, PEER_HALF, tl), lambda i: (0, 0, i))
    shape = jax.ShapeDtypeStruct((PEER_HEADS, PEER_HALF, t), F32)
    return pl.pallas_call(
        _route_kernel,
        grid=(t // tl,),
        in_specs=[pl.BlockSpec((tl, d), lambda i: (i, 0)), const(wq_hi), const(wq_lo), const(key_hi),
                  const(key_lo)],
        out_specs=[out] * 4,
        out_shape=[shape] * 4,
        compiler_params=_cp("parallel"),
    )(h2, wq_hi, wq_lo, key_hi, key_lo)


def _experts_kernel(h_ref, u_ref, vt_ref, r1_ref, e1_ref, l0_ref, a0_ref, x_ref, g_ref, y_out, acc, *,
                    final_norm):
    ee = pl.program_id(1)

    @pl.when(ee == 0)
    def _():
        acc[...] = jnp.zeros_like(acc)

    te = u_ref.shape[0]
    pre = lax.dot_general(u_ref[...], h_ref[...].astype(BF16), NT, preferred_element_type=F32)
    act = 0.5 * pre * (1.0 + lax.erf(pre * (2.0 ** -0.5)))
    parts = []
    for s in range(te // PEER_HALF):
        i0 = ee * (te // PEER_HALF) + s
        gate = jnp.zeros((PEER_HALF, act.shape[1]), F32)
        for m in range(PEER_HEADS):
            picks = l0_ref[m, pl.ds(i0, 1), :]
            coef = a0_ref[m, pl.ds(i0, 1), :]
            gate = gate + jnp.where(r1_ref[m] < picks, e1_ref[m], 0.0) * coef
        parts.append((gate * act[s * PEER_HALF:(s + 1) * PEER_HALF, :]).astype(BF16))
    acc[...] += _dot(vt_ref[...], jnp.concatenate(parts, axis=0))

    @pl.when(ee == pl.num_programs(1) - 1)
    def _():
        y = x_ref[...] + acc[...].T
        if final_norm:
            y = y * lax.rsqrt(jnp.mean(y * y, axis=-1, keepdims=True) + EPS) * g_ref[...]
        y_out[...] = y


def _experts(h2, u_bf, vt_bf, r1, e1, l0, a0, x2, g, tl, te, final_norm):
    t, d = x2.shape
    n = u_bf.shape[0]
    route = pl.BlockSpec((PEER_HEADS, PEER_HALF, tl), lambda i, e: (0, 0, i))
    row = pl.BlockSpec((tl, d), lambda i, e: (i, 0))
    return pl.pallas_call(
        functools.partial(_experts_kernel, final_norm=final_norm),
        grid=(t // tl, n // te),
        in_specs=[row, pl.BlockSpec((te, d), lambda i, e: (e, 0)), pl.BlockSpec((d, te), lambda i, e: (0, e)),
                  route, route, route, route, row, pl.BlockSpec((1, d), lambda i, e: (0, 0))],
        out_specs=row,
        out_shape=jax.ShapeDtypeStruct((t, d), F32),
        scratch_shapes=[pltpu.VMEM((d, tl), F32)],
        compiler_params=_cp("parallel", "arbitrary"),
    )(h2, u_bf, vt_bf, r1, e1, l0, a0, x2, g)


def _pick(total, prefs):
    for p in prefs:
        if total % p == 0:
            return p
    raise ValueError(f"no tile of {prefs} divides {total}")


def kernel(x_prompt, x_sample, cache_k, cache_v, state_conv, state_C, state_n, state_m, page_table, meta_tokens, norm_mix, w_in, b_in, sb_bias, conv_w, ml_norm, w_sb, w_cv, w_ml, w_out, norm_ffn, peer_wq, peer_subkeys, peer_u, peer_v, norm_final):
    nbp, seq, d = x_prompt.shape
    nbs, dec_seq, _ = x_sample.shape
    depth, n_pool = cache_k.shape[:2]
    assert dec_seq == 1 and d % 512 == 0 and peer_subkeys.shape[2:] == (PEER_HALF, PEER_HALF)
    assert cache_k.shape[2:] == (PAGE_SIZE, SB_HEADS, SB_HEAD_DIM)
    n_keys = peer_subkeys.shape[2]
    seq_len = N_META + seq
    lp = -(-seq_len // BLK) * BLK
    pad0 = lp - seq_len
    tp = nbp * lp
    t_all = -(-(tp + nbs) // 512) * 512

    names = ("ga", "gc", "gm", "sq", "sk", "sv", "cb", "cc", "cx", "mq", "mk", "mv", "mo", "if")
    widths = (d,) * 3 + (512,) * 10 + (BLK,)
    cols, off = {}, 0
    for nme, wd in zip(names, widths):
        cols[nme] = off
        off += wd
    n_mix = off
    src = 3 * SB_WIDTH + 3 * CONV_WIDTH + 4 * ML_WIDTH

    def widen(w):
        pad = [(0, 0)] * (w.ndim - 1) + [(0, BLK - 2 * ML_HEADS)]
        return jnp.concatenate([w[..., src + 2 * ML_HEADS:], w[..., :src],
                                jnp.pad(w[..., src:src + 2 * ML_HEADS], pad)], axis=-1)

    w_in_p = widen(w_in).astype(BF16)
    b_in_p = widen(b_in)[:, None, :]

    meta = jnp.broadcast_to(meta_tokens[None], (nbp, N_META, d))
    xp = jnp.concatenate([jnp.zeros((nbp, pad0, d), F32), meta, x_prompt], axis=1).reshape(tp, d)
    x = jnp.concatenate([xp, x_sample.reshape(nbs, d), jnp.zeros((t_all - tp - nbs, d), F32)], axis=0)

    ck = cache_k.reshape(depth * n_pool, PAGE_SIZE, SB_WIDTH)
    cv = cache_v.reshape(depth * n_pool, PAGE_SIZE, SB_WIDTH)
    n_pages = page_table.shape[1]
    npg = _pick(n_pages, (8, 4, 2, 1))
    tm_mix = _pick(t_all, (1024, 512))
    tn_mix = _pick(n_mix, (1664, 1280, 640, 128))
    tl_exp = 512
    te_exp = 512
    tail = jnp.zeros((t_all - tp - nbs, 512), F32)

    outs_p, outs_s = [], []
    for l in range(depth):
        z = _mix_in(x, norm_mix[l][None], w_in_p[l], b_in_p[l], tm_mix, tn_mix)
        zs = z[tp:tp + nbs].reshape(nbs, 1, n_mix)

        a_p = _sb_prompt(z, sb_bias[l], nbp, lp, pad0, cols["sq"], cols["sk"], cols["sv"])
        bias_b = jnp.broadcast_to(sb_bias[l][:, None], (SB_HEADS, BLK))
        a_s = _sb_decode(page_table, bias_b, zs[:, :, cols["sq"]:cols["sq"] + SB_WIDTH], ck, cv, l, n_pool, npg)

        gain = ml_norm[l][None]
        c_p, hm_p, cst_p, cs_p, ns_p, ms_p = _ml_prompt(z, conv_w[l], gain, nbp, lp, pad0, cols)
        m_b = jnp.broadcast_to(state_m[l][:, :, None], (nbs, ML_HEADS, BLK))
        c_s, hm_s, cst_s, cs_s, ns_s, ms_s = _ml_sample(zs, conv_w[l], gain, state_conv[l], state_C[l],
                                                          state_n[l], m_b, cols)

        a = jnp.concatenate([a_p, a_s.reshape(nbs, SB_WIDTH), tail], axis=0)
        c = jnp.concatenate([c_p, c_s.reshape(nbs, CONV_WIDTH), tail], axis=0)
        hm = jnp.concatenate([hm_p, hm_s.reshape(nbs, ML_WIDTH), tail], axis=0)
        x2, h2 = _merge(x, a, c, hm, z, w_sb[l].astype(BF16), w_cv[l].astype(BF16), w_ml[l].astype(BF16),
                        w_out[l].astype(BF16), norm_ffn[l][None], 512)

        wq_t = peer_wq[l].T
        wq_hi = wq_t.astype(BF16)
        wq_lo = (wq_t - wq_hi.astype(F32)).astype(BF16)
        key_hi = peer_subkeys[l].astype(BF16)
        key_lo = (peer_subkeys[l] - key_hi.astype(F32)).astype(BF16)
        r1, e1, l0, a0 = _route(h2, wq_hi, wq_lo, key_hi, key_lo, 256)
        x = _experts(h2, peer_u[l].astype(BF16), peer_v[l].T.astype(BF16), r1, e1, l0, a0, x2,
                     norm_final[None], tl_exp, te_exp, final_norm=(l == depth - 1))

        zp = z[:tp].reshape(nbp, lp, n_mix)[:, pad0:]
        shp = (nbp, seq_len, SB_HEADS, SB_HEAD_DIM)
        outs_p.append((zp[:, :, cols["sk"]:cols["sk"] + SB_WIDTH].reshape(shp),
                       zp[:, :, cols["sv"]:cols["sv"] + SB_WIDTH].reshape(shp),
                       cst_p[:, 8 - (CONV_K - 1):], cs_p, ns_p, ms_p[:, :, 0]))
        shs = (nbs, 1, SB_HEADS, SB_HEAD_DIM)
        outs_s.append((zs[:, :, cols["sk"]:cols["sk"] + SB_WIDTH].reshape(shs),
                       zs[:, :, cols["sv"]:cols["sv"] + SB_WIDTH].reshape(shs),
                       cst_s, cs_s, ns_s, ms_s[:, :, 0]))

    y_prompt = x[:tp].reshape(nbp, lp, d)[:, pad0 + N_META:]
    y_sample = x[tp:tp + nbs].reshape(nbs, 1, d)
    stack = lambda outs: tuple(jnp.stack([o[i] for o in outs]) for i in range(6))
    return (y_prompt, y_sample) + stack(outs_p) + stack(outs_s)
```

```python
import functools

import jax
import jax.numpy as jnp
from jax import lax
from jax.experimental import pallas as pl
from jax.experimental.pallas import tpu as pltpu

N_META = 16
SB_HEADS = 8
SB_HEAD_DIM = 64
SB_WIDTH = SB_HEADS * SB_HEAD_DIM
CONV_WIDTH = 512
CONV_K = 3
ML_HEADS = 4
ML_HEAD_DIM = 128
ML_WIDTH = ML_HEADS * ML_HEAD_DIM
PEER_HEADS = 8
PEER_TOPK = 16
PEER_HALF = 128
PAGE_SIZE = 128
EPS = 1e-6

BLK = 128
SB_TILE = 256
SB_GROUP = 4
PEER_SUB = 256
NEG = -1e30
VMEM_LIMIT = 56 * 1024 * 1024

F32 = jnp.float32
BF16 = jnp.bfloat16
NT = (((1,), (1,)), ((), ()))


def _cp(*sem):
    return pltpu.CompilerParams(dimension_semantics=sem, vmem_limit_bytes=VMEM_LIMIT)


def _split(x):
    hi = x.astype(BF16)
    lo = (x - hi.astype(F32)).astype(BF16)
    return hi, lo


def _softplus(z):
    return jnp.maximum(z, 0.0) + jnp.log(1.0 + jnp.exp(-jnp.abs(z)))


def _log_sigmoid(z):
    return jnp.minimum(z, 0.0) - jnp.log(1.0 + jnp.exp(-jnp.abs(z)))


def _dot(a, b):
    return jnp.dot(a, b, preferred_element_type=F32)


def _tri(n, lower):
    r = lax.broadcasted_iota(jnp.int32, (n, n), 0)
    c = lax.broadcasted_iota(jnp.int32, (n, n), 1)
    return jnp.where((r >= c) if lower else (r <= c), 1.0, 0.0).astype(BF16)


def _mix_kernel(x_ref, g_ref, w_ref, b_ref, z_ref, h_scr):
    @pl.when(pl.program_id(1) == 0)
    def _():
        x = x_ref[...]
        y = x * lax.rsqrt(jnp.mean(x * x, axis=-1, keepdims=True) + EPS)
        h_scr[...] = (y * g_ref[...]).astype(BF16)

    z_ref[...] = _dot(h_scr[...], w_ref[...]) + b_ref[...]


def _mix_in(x, g, w, b, tm, tn):
    t, d = x.shape
    n = w.shape[1]
    return pl.pallas_call(
        _mix_kernel,
        grid=(t // tm, n // tn),
        in_specs=[
            pl.BlockSpec((tm, d), lambda i, j: (i, 0)),
            pl.BlockSpec((1, d), lambda i, j: (0, 0)),
            pl.BlockSpec((d, tn), lambda i, j: (0, j)),
            pl.BlockSpec((1, tn), lambda i, j: (0, j)),
        ],
        out_specs=pl.BlockSpec((tm, tn), lambda i, j: (i, j)),
        out_shape=jax.ShapeDtypeStruct((t, n), F32),
        scratch_shapes=[pltpu.VMEM((tm, d), BF16)],
        compiler_params=_cp("parallel", "arbitrary"),
    )(x, g, w, b)


def _sb_block(qh, bh, kvs, mincl, vis, carry):
    out = []
    for h in range(len(qh)):
        acc, run = carry[2 * h], carry[2 * h + 1]
        kb, vb = kvs[h // 2]
        z = lax.dot_general(qh[h], kb, NT, preferred_element_type=F32) + bh[h]
        ls = -_softplus(z)
        if vis is not None:
            ls = jnp.where(vis, ls, 0.0)
        hi, lo = _split(ls)
        incl = _dot(hi, mincl) + _dot(lo, mincl)
        w = jnp.exp(z + incl + run)
        if vis is not None:
            w = jnp.where(vis, w, 0.0)
        out += [acc + _dot(w.astype(BF16), vb), run + incl[:, 0:1]]
    return tuple(out)


def _sb_prompt_kernel(bias_ref, q_ref, k_ref, v_ref, o_ref, *, seq):
    hg = pl.program_id(1)
    qi = pl.program_id(2)
    nq = seq // SB_TILE
    pairs = SB_GROUP // 2
    bh = [bias_ref[SB_GROUP * hg + h] for h in range(SB_GROUP)]
    first_head = lax.broadcasted_iota(jnp.int32, (1, BLK), 1) < SB_HEAD_DIM

    def heads(off, n):
        out = []
        for p in range(pairs):
            q = q_ref[pl.ds(off, n), p * BLK:(p + 1) * BLK] * (SB_HEAD_DIM ** -0.5)
            out += [jnp.where(first_head, q, 0.0).astype(BF16), jnp.where(first_head, 0.0, q).astype(BF16)]
        return out

    def keys(off, n):
        return [(k_ref[pl.ds(off, n), p * BLK:(p + 1) * BLK].astype(BF16),
                 v_ref[pl.ds(off, n), p * BLK:(p + 1) * BLK].astype(BF16)) for p in range(pairs)]

    def start(rows):
        return (jnp.zeros((rows, BLK), F32), jnp.zeros((rows, 1), F32)) * SB_GROUP

    def store(off, n, carry):
        for p in range(pairs):
            o_ref[pl.ds(off, n), p * BLK:(p + 1) * BLK] = jnp.where(first_head, carry[4 * p], carry[4 * p + 2])

    @pl.when(qi < nq)
    def _():
        r0 = pl.multiple_of(qi * SB_TILE, SB_TILE)
        qh = heads(r0, SB_TILE)
        mincl = _tri(SB_TILE, True)
        rowi = lax.broadcasted_iota(jnp.int32, (SB_TILE, SB_TILE), 0)
        coli = lax.broadcasted_iota(jnp.int32, (SB_TILE, SB_TILE), 1)
        carry = _sb_block(qh, bh, keys(r0, SB_TILE), mincl, coli < rowi, start(SB_TILE))

        def body(i, c):
            off = pl.multiple_of((qi - 1 - i) * SB_TILE, SB_TILE)
            return _sb_block(qh, bh, keys(off, SB_TILE), mincl, None, c)

        carry = lax.fori_loop(0, qi, body, carry)
        meta_vis = lax.broadcasted_iota(jnp.int32, (SB_TILE, BLK), 1) < N_META
        carry = _sb_block(qh, bh, keys(seq, BLK), _tri(BLK, True), meta_vis, carry)
        store(r0, SB_TILE, carry)

    @pl.when(qi == nq)
    def _():
        rowi = lax.broadcasted_iota(jnp.int32, (BLK, BLK), 0)
        coli = lax.broadcasted_iota(jnp.int32, (BLK, BLK), 1)
        vis = (coli < rowi) & (coli < N_META)
        store(seq, BLK, _sb_block(heads(seq, BLK), bh, keys(seq, BLK), _tri(BLK, True), vis, start(BLK)))


def _sb_prompt(z, bias, nb_batch, lp, seq, col_q, col_k, col_v):
    wd = SB_GROUP * SB_HEAD_DIM
    spec = lambda col: pl.BlockSpec((lp, wd), lambda b, h, i: (b, col // wd + h))
    return pl.pallas_call(
        functools.partial(_sb_prompt_kernel, seq=seq),
        grid=(nb_batch, SB_HEADS // SB_GROUP, seq // SB_TILE + 1),
        in_specs=[pl.BlockSpec(memory_space=pltpu.SMEM), spec(col_q), spec(col_k), spec(col_v)],
        out_specs=pl.BlockSpec((lp, wd), lambda b, h, i: (b, h)),
        out_shape=jax.ShapeDtypeStruct((nb_batch * lp, SB_WIDTH), F32),
        compiler_params=_cp("parallel", "parallel", "arbitrary"),
    )(bias, z, z, z)


def _sb_decode_kernel(pt_ref, bias_ref, q_ref, *refs, npg):
    del pt_ref
    k_refs, v_refs = refs[:npg], refs[npg:2 * npg]
    o_ref, acc_s, run_s = refs[2 * npg:]
    j = pl.program_id(1)

    @pl.when(j == 0)
    def _():
        acc_s[...] = jnp.zeros_like(acc_s)
        run_s[...] = jnp.zeros_like(run_s)

    hrow = lax.broadcasted_iota(jnp.int32, (SB_HEADS, SB_WIDTH), 0)
    lane = lax.broadcasted_iota(jnp.int32, (SB_HEADS, SB_WIDTH), 1)
    own = (lane >= hrow * SB_HEAD_DIM) & (lane < (hrow + 1) * SB_HEAD_DIM)
    q = q_ref[0] * (SB_HEAD_DIM ** -0.5)
    qbd = jnp.where(own, jnp.broadcast_to(q, (SB_HEADS, SB_WIDTH)), 0.0).astype(BF16)
    mincl = _tri(BLK, True)
    bias = bias_ref[...]
    acc = acc_s[...]
    run = run_s[...]
    for i in range(npg):
        kp = k_refs[i][0].astype(BF16)
        vp = v_refs[i][0].astype(BF16)
        z = lax.dot_general(qbd, kp, NT, preferred_element_type=F32) + bias
        hi, lo = _split(-_softplus(z))
        incl = _dot(hi, mincl) + _dot(lo, mincl)
        w = jnp.exp(z + incl + run).astype(BF16)
        acc = acc + _dot(w, vp)
        run = run + jnp.broadcast_to(incl[:, 0:1], run.shape)
    acc_s[...] = acc
    run_s[...] = run

    @pl.when(j == pl.num_programs(1) - 1)
    def _():
        o_ref[0] = jnp.sum(jnp.where(own, acc, 0.0), axis=0, keepdims=True)


def _sb_decode(page_table, bias_b, q, cache_k, cache_v, layer, n_pool, npg):
    nb, n_pages = page_table.shape
    steps = n_pages // npg

    def page_spec(i):
        def index(b, j, pt):
            return (layer * n_pool + pt[b, n_pages - 1 - (j * npg + i)], 0, 0)
        return pl.BlockSpec((1, PAGE_SIZE, SB_WIDTH), index)

    grid_spec = pltpu.PrefetchScalarGridSpec(
        num_scalar_prefetch=1,
        grid=(nb, steps),
        in_specs=[
            pl.BlockSpec((SB_HEADS, BLK), lambda b, j, pt: (0, 0)),
            pl.BlockSpec((1, 1, SB_WIDTH), lambda b, j, pt: (b, 0, 0)),
        ] + [page_spec(i) for i in range(npg)] * 2,
        out_specs=pl.BlockSpec((1, 1, SB_WIDTH), lambda b, j, pt: (b, 0, 0)),
        scratch_shapes=[pltpu.VMEM((SB_HEADS, SB_WIDTH), F32), pltpu.VMEM((SB_HEADS, BLK), F32)],
    )
    return pl.pallas_call(
        functools.partial(_sb_decode_kernel, npg=npg),
        grid_spec=grid_spec,
        out_shape=jax.ShapeDtypeStruct((nb, 1, SB_WIDTH), F32),
        compiler_params=_cp("parallel", "arbitrary"),
    )(page_table, bias_b, q, *([cache_k] * npg), *([cache_v] * npg))


def _ml_out(hh, gain, o):
    hn = hh * lax.rsqrt(jnp.mean(hh * hh, axis=-1, keepdims=True) + EPS)
    return hn * gain * jax.nn.sigmoid(o)


def _ml_prompt_kernel(cb_ref, cc_ref, cx_ref, q_ref, k_ref, v_ref, o_ref, zif_ref, cw_ref, gain_ref,
                      c_out, hm_out, cst_out, cs_out, ns_out, ms_out, ubuf, c_s, n_s, m_s):
    ci = pl.program_id(1)

    @pl.when(ci == 0)
    def _():
        ubuf[0:8, :] = jnp.zeros((8, CONV_WIDTH), F32)
        c_s[...] = jnp.zeros_like(c_s)
        n_s[...] = jnp.zeros_like(n_s)
        m_s[...] = jnp.zeros_like(m_s)

    real_c = (ci > 0) | (lax.broadcasted_iota(jnp.int32, (BLK, 1), 0) < N_META)
    real_r = (ci > 0) | (lax.broadcasted_iota(jnp.int32, (1, BLK), 1) < N_META)

    u = jnp.where(real_c, cc_ref[...] * cx_ref[...], 0.0)
    ubuf[8:8 + BLK, :] = u
    y = cw_ref[0:1, :] * ubuf[6:6 + BLK, :] + cw_ref[1:2, :] * ubuf[7:7 + BLK, :] + cw_ref[2:3, :] * u
    c_out[...] = cb_ref[...] * y
    last = jnp.where(ci == 0, u[N_META - 8:N_META, :], u[BLK - 8:, :])
    ubuf[0:8, :] = last
    cst_out[0] = last

    ltri = _tri(BLK, True)
    utri = _tri(BLK, False)
    causal = lax.broadcasted_iota(jnp.int32, (BLK, BLK), 1) <= lax.broadcasted_iota(jnp.int32, (BLK, BLK), 0)
    zif = zif_ref[...]
    zif_t = zif.T
    scale = ML_HEAD_DIM ** -0.5
    for h in range(ML_HEADS):
        sl = slice(h * ML_HEAD_DIM, (h + 1) * ML_HEAD_DIM)
        li_c = jnp.where(real_c, zif[:, h:h + 1], NEG)
        lf_c = jnp.where(real_c, _log_sigmoid(zif[:, ML_HEADS + h:ML_HEADS + h + 1]), 0.0)
        li_r = jnp.where(real_r, zif_t[h:h + 1, :], NEG)
        lf_r = jnp.where(real_r, _log_sigmoid(zif_t[ML_HEADS + h:ML_HEADS + h + 1, :]), 0.0)
        ch, cl = _split(jnp.broadcast_to(lf_c, (BLK, BLK)))
        bcol = _dot(ltri, ch) + _dot(ltri, cl)
        rh, rl = _split(jnp.broadcast_to(lf_r, (BLK, BLK)))
        brow = _dot(rh, utri) + _dot(rl, utri)
        m_b = m_s[h]
        dmat = jnp.where(causal, bcol - brow + li_r, NEG)
        mt = jnp.maximum(bcol + m_b, jnp.max(dmat, axis=1, keepdims=True))
        qf = q_ref[:, sl]
        kf = k_ref[:, sl] * scale
        qb, kb, vb = qf.astype(BF16), kf.astype(BF16), v_ref[:, sl].astype(BF16)
        p = lax.dot_general(qb, kb, NT, preferred_element_type=F32) * jnp.exp(dmat - mt)
        w_inter = jnp.exp(bcol + m_b - mt)
        c_old = c_s[h]
        n_old = n_s[h]
        num = _dot(p.astype(BF16), vb) + w_inter * _dot(qb, c_old.astype(BF16))
        den = (jnp.sum(p, axis=1, keepdims=True)
               + w_inter[:, 0:1] * jnp.sum(qf * n_old, axis=1, keepdims=True))
        hh = num / jnp.maximum(jnp.abs(den), jnp.exp(-mt[:, 0:1]))
        hm_out[:, sl] = _ml_out(hh, gain_ref[:, sl], o_ref[:, sl])

        bend = bcol[BLK - 1:BLK, :]
        log_row = bend - brow[0:1, :] + li_r
        m_new = jnp.maximum(bend + m_b, jnp.max(log_row, axis=1, keepdims=True))
        a_prev = jnp.exp(bend + m_b - m_new)
        ka = kf * jnp.exp(bend - bcol + li_c - m_new)
        c_new = a_prev * c_old + _dot(ka.T.astype(BF16), vb)
        n_new = a_prev * n_old + jnp.sum(ka, axis=0, keepdims=True)
        c_s[h] = c_new
        n_s[h] = n_new
        m_s[h] = m_new
        cs_out[0, h] = c_new
        ns_out[0, h:h + 1, :] = n_new
        ms_out[0, h:h + 1, :] = m_new


def _ml_prompt(z, conv_w, gain, nb_batch, lp, cols):
    nb = lp // BLK
    blk_of = lambda b, i: b * nb + (i + nb - 1) % nb

    def zspec(col, width):
        return pl.BlockSpec((BLK, width), lambda b, i: (blk_of(b, i), col // width))

    row_spec = pl.BlockSpec((BLK, ML_WIDTH), lambda b, i: (blk_of(b, i), 0))
    return pl.pallas_call(
        _ml_prompt_kernel,
        grid=(nb_batch, nb),
        in_specs=[zspec(cols[n], 512) for n in ("cb", "cc", "cx", "mq", "mk", "mv", "mo")] + [
            zspec(cols["if"], BLK),
            pl.BlockSpec((CONV_K, CONV_WIDTH), lambda b, i: (0, 0)),
            pl.BlockSpec((1, ML_WIDTH), lambda b, i: (0, 0)),
        ],
        out_specs=[
            row_spec, row_spec,
            pl.BlockSpec((1, 8, CONV_WIDTH), lambda b, i: (b, 0, 0)),
            pl.BlockSpec((1, ML_HEADS, BLK, BLK), lambda b, i: (b, 0, 0, 0)),
            pl.BlockSpec((1, ML_HEADS, BLK), lambda b, i: (b, 0, 0)),
            pl.BlockSpec((1, ML_HEADS, BLK), lambda b, i: (b, 0, 0)),
        ],
        out_shape=[
            jax.ShapeDtypeStruct((nb_batch * lp, CONV_WIDTH), F32),
            jax.ShapeDtypeStruct((nb_batch * lp, ML_WIDTH), F32),
            jax.ShapeDtypeStruct((nb_batch, 8, CONV_WIDTH), F32),
            jax.ShapeDtypeStruct((nb_batch, ML_HEADS, BLK, BLK), F32),
            jax.ShapeDtypeStruct((nb_batch, ML_HEADS, BLK), F32),
            jax.ShapeDtypeStruct((nb_batch, ML_HEADS, BLK), F32),
        ],
        scratch_shapes=[
            pltpu.VMEM((8 + BLK, CONV_WIDTH), F32),
            pltpu.VMEM((ML_HEADS, BLK, BLK), F32),
            pltpu.VMEM((ML_HEADS, 1, BLK), F32),
            pltpu.VMEM((ML_HEADS, 1, BLK), F32),
        ],
        compiler_params=_cp("parallel", "arbitrary"),
    )(z, z, z, z, z, z, z, z, conv_w, gain)


def _ml_sample_kernel(cb_ref, cc_ref, cx_ref, q_ref, k_ref, v_ref, o_ref, zif_ref, cw_ref, gain_ref,
                      sc_ref, c_ref, n_ref, m_ref, c_out, hm_out, sc_out, cs_out, ns_out, ms_out):
    u = cc_ref[0] * cx_ref[0]
    prev = sc_ref[0]
    y = cw_ref[0:1, :] * prev[0:1, :] + cw_ref[1:2, :] * prev[1:2, :] + cw_ref[2:3, :] * u
    c_out[0] = cb_ref[0] * y
    sc_out[0, 0:1, :] = prev[1:2, :]
    sc_out[0, 1:2, :] = u

    zif = zif_ref[0]
    q_all, k_all, v_all, o_all = q_ref[0], k_ref[0], v_ref[0], o_ref[0]
    scale = ML_HEAD_DIM ** -0.5
    for h in range(ML_HEADS):
        sl = slice(h * ML_HEAD_DIM, (h + 1) * ML_HEAD_DIM)
        li = zif[:, h:h + 1]
        lf = _log_sigmoid(zif[:, ML_HEADS + h:ML_HEADS + h + 1])
        m_b = m_ref[0, h:h + 1, :]
        m_new = jnp.maximum(lf + m_b, li)
        a_prev = jnp.exp(lf + m_b - m_new)
        a_in = jnp.exp(li - m_new)
        q_r, k_r, v_r = q_all[:, sl], k_all[:, sl] * scale, v_all[:, sl]
        k_c = jnp.broadcast_to(k_r, (BLK, BLK)).T
        q_c = jnp.broadcast_to(q_r, (BLK, BLK)).T
        c_new = a_prev * c_ref[0, h] + (a_in * k_c) * v_r
        n_new = a_prev * n_ref[0, h:h + 1, :] + a_in * k_r
        num = jnp.sum(q_c * c_new, axis=0, keepdims=True)
        den = jnp.sum(q_r * n_new, axis=1, keepdims=True)
        hh = num / jnp.maximum(jnp.abs(den), jnp.exp(-m_new))
        hm_out[0, :, sl] = _ml_out(hh, gain_ref[:, sl], o_all[:, sl])
        cs_out[0, h] = c_new
        ns_out[0, h:h + 1, :] = n_new
        ms_out[0, h:h + 1, :] = m_new


def _ml_sample(zs, conv_w, gain, st_conv, st_c, st_n, st_m, cols):
    nb = zs.shape[0]

    def zspec(col, width):
        return pl.BlockSpec((1, 1, width), lambda b: (b, 0, col // width))

    def full(*shape):
        return pl.BlockSpec((1,) + shape, lambda b: (b,) + (0,) * len(shape))

    return pl.pallas_call(
        _ml_sample_kernel,
        grid=(nb,),
        in_specs=[zspec(cols[n], 512) for n in ("cb", "cc", "cx", "mq", "mk", "mv", "mo")] + [
            zspec(cols["if"], BLK),
            pl.BlockSpec((CONV_K, CONV_WIDTH), lambda b: (0, 0)),
            pl.BlockSpec((1, ML_WIDTH), lambda b: (0, 0)),
            full(CONV_K - 1, CONV_WIDTH), full(ML_HEADS, BLK, BLK), full(ML_HEADS, BLK), full(ML_HEADS, BLK),
        ],
        out_specs=[full(1, CONV_WIDTH), full(1, ML_WIDTH), full(CONV_K - 1, CONV_WIDTH),
                   full(ML_HEADS, BLK, BLK), full(ML_HEADS, BLK), full(ML_HEADS, BLK)],
        out_shape=[
            jax.ShapeDtypeStruct((nb, 1, CONV_WIDTH), F32),
            jax.ShapeDtypeStruct((nb, 1, ML_WIDTH), F32),
            jax.ShapeDtypeStruct((nb, CONV_K - 1, CONV_WIDTH), F32),
            jax.ShapeDtypeStruct((nb, ML_HEADS, BLK, BLK), F32),
            jax.ShapeDtypeStruct((nb, ML_HEADS, BLK), F32),
            jax.ShapeDtypeStruct((nb, ML_HEADS, BLK), F32),
        ],
        compiler_params=_cp("parallel"),
    )(zs, zs, zs, zs, zs, zs, zs, zs, conv_w, gain, st_conv, st_c, st_n, st_m)


def _merge_kernel(x_ref, a_ref, c_ref, hm_ref, ga_ref, gc_ref, gm_ref, wsb_ref, wcv_ref, wml_ref,
                  wout_ref, nf_ref, x2_out, h2_out):
    def proj(t_ref, w_ref):
        return _dot(t_ref[...].astype(BF16), w_ref[...])

    merged = (jax.nn.sigmoid(ga_ref[...]) * proj(a_ref, wsb_ref)
              + jax.nn.sigmoid(gc_ref[...]) * proj(c_ref, wcv_ref)
              + jax.nn.sigmoid(gm_ref[...]) * proj(hm_ref, wml_ref))
    x2 = x_ref[...] + _dot(merged.astype(BF16), wout_ref[...])
    x2_out[...] = x2
    h2_out[...] = x2 * lax.rsqrt(jnp.mean(x2 * x2, axis=-1, keepdims=True) + EPS) * nf_ref[...]


def _merge(x, a, c, hm, z, wsb, wcv, wml, wout, nf, tm):
    t, d = x.shape
    row = lambda w: pl.BlockSpec((tm, w), lambda i: (i, 0))
    const = lambda arr: pl.BlockSpec(arr.shape, lambda i: (0, 0))
    gate = lambda g: pl.BlockSpec((tm, d), lambda i: (i, g))
    return pl.pallas_call(
        _merge_kernel,
        grid=(t // tm,),
        in_specs=[row(d), row(SB_WIDTH), row(CONV_WIDTH), row(ML_WIDTH), gate(0), gate(1), gate(2),
                  const(wsb), const(wcv), const(wml), const(wout), const(nf)],
        out_specs=[row(d), row(d)],
        out_shape=[jax.ShapeDtypeStruct((t, d), F32), jax.ShapeDtypeStruct((t, d), F32)],
        compiler_params=_cp("parallel"),
    )(x, a, c, hm, z, z, z, wsb, wcv, wml, wout, nf)


def _top_sorted(x, k):
    n, cols = x.shape
    rows = lax.broadcasted_iota(jnp.int32, (n, cols), 0).astype(F32)
    rank = jnp.full((n, cols), float(k), F32)
    vals = []
    for r in range(k):
        mx = jnp.max(x, axis=0, keepdims=True)
        first = jnp.min(jnp.where(x == mx, rows, float(n)), axis=0, keepdims=True)
        sel = rows == first
        rank = jnp.where(sel, float(r), rank)
        x = jnp.where(sel, -jnp.inf, x)
        vals.append(mx)
    return vals, rank


def _route_kernel(h_ref, wh_ref, wl_ref, kh_ref, kl_ref, r1_out, e1_out, l0_out, a0_out):
    xh, xl = _split(h_ref[...])
    dg = functools.partial(lax.dot_general, dimension_numbers=NT, preferred_element_type=F32)
    q_t = dg(wh_ref[...], xh) + dg(wl_ref[...], xh) + dg(wh_ref[...], xl)
    cols = q_t.shape[1]
    k = PEER_TOPK
    slot = lax.broadcasted_iota(jnp.int32, (k, cols), 0).astype(F32)
    for m in range(PEER_HEADS):
        sc = []
        for p in range(2):
            qh, ql = _split(q_t[(2 * m + p) * PEER_HALF:(2 * m + p + 1) * PEER_HALF, :])
            sc.append(_dot(kh_ref[p], qh) + _dot(kl_ref[p], qh) + _dot(kh_ref[p], ql))
        v0, rank0 = _top_sorted(sc[0], k)
        v1, rank1 = _top_sorted(sc[1], k)
        s0 = jnp.concatenate(v0, axis=0)
        cnt = jnp.zeros((k, cols), F32)
        front = s0 + v1[0]
        zsum = jnp.zeros((1, cols), F32)
        for _ in range(k):
            mx = jnp.max(front, axis=0, keepdims=True)
            first = jnp.min(jnp.where(front == mx, slot, float(k)), axis=0, keepdims=True)
            cnt = jnp.where(slot == first, cnt + 1.0, cnt)
            nxt = jnp.full((k, cols), -jnp.inf, F32)
            for j in range(1, k):
                nxt = jnp.where(cnt == float(j), v1[j], nxt)
            front = jnp.where(cnt == 0.0, front, s0 + nxt)
            zsum = zsum + jnp.exp(mx - (v0[0] + v1[0]))
        picks = jnp.zeros_like(rank0)
        for i in range(k):
            picks = jnp.where(rank0 == float(i), cnt[i:i + 1, :], picks)
        r1_out[m] = rank1.astype(BF16)
        e1_out[m] = jnp.exp(sc[1] - v1[0]).astype(BF16)
        l0_out[m] = picks
        a0_out[m] = jnp.exp(sc[0] - v0[0]) / zsum


def _route(h2, wq_hi, wq_lo, key_hi, key_lo, tl):
    t, d = h2.shape
    const = lambda arr: pl.BlockSpec(arr.shape, lambda i: (0,) * arr.ndim)
    out = pl.BlockSpec((PEER_HEADS, PEER_HALF, tl), lambda i: (0, 0, i))
    shape = lambda dt: jax.ShapeDtypeStruct((PEER_HEADS, PEER_HALF, t), dt)
    return pl.pallas_call(
        _route_kernel,
        grid=(t // tl,),
        in_specs=[pl.BlockSpec((tl, d), lambda i: (i, 0)), const(wq_hi), const(wq_lo), const(key_hi),
                  const(key_lo)],
        out_specs=[out] * 4,
        out_shape=[shape(BF16), shape(BF16), shape(F32), shape(F32)],
        compiler_params=_cp("parallel"),
    )(h2, wq_hi, wq_lo, key_hi, key_lo)


def _transpose_kernel(x_ref, o_ref):
    o_ref[...] = x_ref[...].T.astype(o_ref.dtype)


def _transpose_bf16(x, tr):
    n, d = x.shape
    return pl.pallas_call(
        _transpose_kernel,
        grid=(n // tr,),
        in_specs=[pl.BlockSpec((tr, d), lambda i: (i, 0))],
        out_specs=pl.BlockSpec((d, tr), lambda i: (0, i)),
        out_shape=jax.ShapeDtypeStruct((d, n), BF16),
        compiler_params=_cp("parallel"),
    )(x)


def _experts_kernel(h_ref, u_ref, vt_ref, r1_ref, e1_ref, l0_ref, a0_ref, x_ref, g_ref, y_out, acc, *,
                    final_norm):
    ee = pl.program_id(1)

    @pl.when(ee == 0)
    def _():
        acc[...] = jnp.zeros_like(acc)

    te = u_ref.shape[0]
    tl = h_ref.shape[0]
    hb = h_ref[...].astype(BF16)
    pack_rows = 16

    def row(ref, m, i0):
        tile = jnp.broadcast_to(ref[m, pl.ds(i0, 1), :], (pack_rows, tl)).astype(BF16)
        return jnp.concatenate([tile] * (PEER_HALF // pack_rows), axis=0)

    total = None
    for sb in range(te // PEER_SUB):
        rows = slice(sb * PEER_SUB, (sb + 1) * PEER_SUB)
        pre = lax.dot_general(u_ref[rows, :], hb, NT, preferred_element_type=F32)
        act = (0.5 * pre * (1.0 + lax.erf(pre * (2.0 ** -0.5)))).astype(BF16)
        parts = []
        for s in range(PEER_SUB // PEER_HALF):
            i0 = (ee * te + sb * PEER_SUB) // PEER_HALF + s
            gate = jnp.zeros((PEER_HALF, tl), BF16)
            for m in range(PEER_HEADS):
                picked = r1_ref[m] < row(l0_ref, m, i0)
                gate = gate + jnp.where(picked, e1_ref[m], jnp.zeros((), BF16)) * row(a0_ref, m, i0)
            parts.append(gate * act[s * PEER_HALF:(s + 1) * PEER_HALF, :])
        part = _dot(vt_ref[:, rows], jnp.concatenate(parts, axis=0))
        total = part if total is None else total + part
    acc[...] += total

    @pl.when(ee == pl.num_programs(1) - 1)
    def _():
        y = x_ref[...] + acc[...].T
        if final_norm:
            y = y * lax.rsqrt(jnp.mean(y * y, axis=-1, keepdims=True) + EPS) * g_ref[...]
        y_out[...] = y


def _experts(h2, u_bf, vt_bf, r1, e1, l0, a0, x2, g, tl, te, final_norm):
    t, d = x2.shape
    n = u_bf.shape[0]
    route = pl.BlockSpec((PEER_HEADS, PEER_HALF, tl), lambda i, e: (0, 0, i))
    row = pl.BlockSpec((tl, d), lambda i, e: (i, 0))
    return pl.pallas_call(
        functools.partial(_experts_kernel, final_norm=final_norm),
        grid=(t // tl, n // te),
        in_specs=[row, pl.BlockSpec((te, d), lambda i, e: (e, 0)), pl.BlockSpec((d, te), lambda i, e: (0, e)),
                  route, route, route, route, row, pl.BlockSpec((1, d), lambda i, e: (0, 0))],
        out_specs=row,
        out_shape=jax.ShapeDtypeStruct((t, d), F32),
        scratch_shapes=[pltpu.VMEM((d, tl), F32)],
        compiler_params=_cp("parallel", "arbitrary"),
    )(h2, u_bf, vt_bf, r1, e1, l0, a0, x2, g)


def _pick(total, prefs):
    for p in prefs:
        if total % p == 0:
            return p
    raise ValueError(f"no tile of {prefs} divides {total}")


def kernel(x_prompt, x_sample, cache_k, cache_v, state_conv, state_C, state_n, state_m, page_table, meta_tokens, norm_mix, w_in, b_in, sb_bias, conv_w, ml_norm, w_sb, w_cv, w_ml, w_out, norm_ffn, peer_wq, peer_subkeys, peer_u, peer_v, norm_final):
    nbp, seq, d = x_prompt.shape
    nbs, dec_seq, _ = x_sample.shape
    depth, n_pool = cache_k.shape[:2]
    assert dec_seq == 1 and d % 512 == 0 and seq % SB_TILE == 0 and N_META <= BLK
    assert peer_subkeys.shape[2:] == (PEER_HALF, PEER_HALF)
    assert cache_k.shape[2:] == (PAGE_SIZE, SB_HEADS, SB_HEAD_DIM)
    seq_len = N_META + seq
    lp = seq + BLK
    tp = nbp * lp
    t_all = -(-(tp + nbs) // 512) * 512

    names = ("ga", "gc", "gm", "sq", "sk", "sv", "cb", "cc", "cx", "mq", "mk", "mv", "mo", "if")
    widths = (d,) * 3 + (512,) * 10 + (BLK,)
    cols, off = {}, 0
    for nme, wd in zip(names, widths):
        cols[nme] = off
        off += wd
    n_mix = off
    src = 3 * SB_WIDTH + 3 * CONV_WIDTH + 4 * ML_WIDTH

    def widen(w):
        pad = [(0, 0)] * (w.ndim - 1) + [(0, BLK - 2 * ML_HEADS)]
        return jnp.concatenate([w[..., src + 2 * ML_HEADS:], w[..., :src],
                                jnp.pad(w[..., src:src + 2 * ML_HEADS], pad)], axis=-1)

    w_in_p = widen(w_in).astype(BF16)
    b_in_p = widen(b_in)[:, None, :]

    meta = jnp.broadcast_to(meta_tokens[None], (nbp, N_META, d))
    xp = jnp.concatenate([x_prompt, meta, jnp.zeros((nbp, BLK - N_META, d), F32)], axis=1).reshape(tp, d)
    x = jnp.concatenate([xp, x_sample.reshape(nbs, d), jnp.zeros((t_all - tp - nbs, d), F32)], axis=0)

    ck = cache_k.reshape(depth * n_pool, PAGE_SIZE, SB_WIDTH)
    cv = cache_v.reshape(depth * n_pool, PAGE_SIZE, SB_WIDTH)
    n_pages = page_table.shape[1]
    npg = _pick(n_pages, (8, 4, 2, 1))
    tm_mix = _pick(t_all, (1024, 512))
    tn_mix = _pick(n_mix, (1664, 1280, 640, 128))
    tl_exp = 512
    te_exp = 1024
    tail = jnp.zeros((t_all - tp - nbs, 512), F32)

    outs_p, outs_s = [], []
    for l in range(depth):
        z = _mix_in(x, norm_mix[l][None], w_in_p[l], b_in_p[l], tm_mix, tn_mix)
        zs = z[tp:tp + nbs].reshape(nbs, 1, n_mix)

        a_p = _sb_prompt(z, sb_bias[l], nbp, lp, seq, cols["sq"], cols["sk"], cols["sv"])
        bias_b = jnp.broadcast_to(sb_bias[l][:, None], (SB_HEADS, BLK))
        a_s = _sb_decode(page_table, bias_b, zs[:, :, cols["sq"]:cols["sq"] + SB_WIDTH], ck, cv, l, n_pool, npg)

        gain = ml_norm[l][None]
        c_p, hm_p, cst_p, cs_p, ns_p, ms_p = _ml_prompt(z, conv_w[l], gain, nbp, lp, cols)
        m_b = jnp.broadcast_to(state_m[l][:, :, None], (nbs, ML_HEADS, BLK))
        c_s, hm_s, cst_s, cs_s, ns_s, ms_s = _ml_sample(zs, conv_w[l], gain, state_conv[l], state_C[l],
                                                          state_n[l], m_b, cols)

        a = jnp.concatenate([a_p, a_s.reshape(nbs, SB_WIDTH), tail], axis=0)
        c = jnp.concatenate([c_p, c_s.reshape(nbs, CONV_WIDTH), tail], axis=0)
        hm = jnp.concatenate([hm_p, hm_s.reshape(nbs, ML_WIDTH), tail], axis=0)
        x2, h2 = _merge(x, a, c, hm, z, w_sb[l].astype(BF16), w_cv[l].astype(BF16), w_ml[l].astype(BF16),
                        w_out[l].astype(BF16), norm_ffn[l][None], 512)

        wq_t = peer_wq[l].T
        wq_hi = wq_t.astype(BF16)
        wq_lo = (wq_t - wq_hi.astype(F32)).astype(BF16)
        key_hi = peer_subkeys[l].astype(BF16)
        key_lo = (peer_subkeys[l] - key_hi.astype(F32)).astype(BF16)
        r1, e1, l0, a0 = _route(h2, wq_hi, wq_lo, key_hi, key_lo, 256)
        x = _experts(h2, peer_u[l].astype(BF16), _transpose_bf16(peer_v[l], 512), r1, e1, l0, a0, x2,
                     norm_final[None], tl_exp, te_exp, final_norm=(l == depth - 1))

        zp = z[:tp].reshape(nbp, lp, n_mix)
        shp = (nbp, seq_len, SB_HEADS, SB_HEAD_DIM)
        in_order = lambda col: jnp.concatenate(
            [zp[:, seq:seq + N_META, col:col + SB_WIDTH], zp[:, :seq, col:col + SB_WIDTH]], axis=1).reshape(shp)
        outs_p.append((in_order(cols["sk"]), in_order(cols["sv"]), cst_p[:, 8 - (CONV_K - 1):], cs_p, ns_p,
                       ms_p[:, :, 0]))
        shs = (nbs, 1, SB_HEADS, SB_HEAD_DIM)
        outs_s.append((zs[:, :, cols["sk"]:cols["sk"] + SB_WIDTH].reshape(shs),
                       zs[:, :, cols["sv"]:cols["sv"] + SB_WIDTH].reshape(shs),
                       cst_s, cs_s, ns_s, ms_s[:, :, 0]))

    y_prompt = x[:tp].reshape(nbp, lp, d)[:, :seq]
    y_sample = x[tp:tp + nbs].reshape(nbs, 1, d)
    stack = lambda outs: tuple(jnp.stack([o[i] for o in outs]) for i in range(6))
    return (y_prompt, y_sample) + stack(outs_p) + stack(outs_s)
```
